```python
import jax, jax.numpy as jnp
from jax import lax
import numpy as np

D_MODEL = 4096
BATCH = 4
SEQ = 4096
DEPTH = 1

CTX_LEN = 256
GRID_W = 64

NA_HEADS = 16
NA_HEAD_DIM = 128
NA_WIDTH = NA_HEADS * NA_HEAD_DIM
WIN_H = 8
WIN_W = 16
SG_GROUPS = 16
SG_GROUP_DIM = 128
SG_WIDTH = SG_GROUPS * SG_GROUP_DIM
CHUNK = 128
MIX_WIDTH = NA_WIDTH + SG_WIDTH
IN_WIDTH = 3 * NA_WIDTH + 2 * SG_WIDTH
N_EXPERTS = 64
TOP_K = 8
N_GROUPS = 8
TOPK_GROUPS = 4
EXPERT_DIM = 512
SHARED_DIM = 512
ROUTED_SCALE = 2.5
EPS = 1e-6

kernel_name = "hybrid_na_sgu_moe_dit_layer"


def rmsnorm(x, g):
    xf = x.astype(jnp.float32)
    y = xf * lax.rsqrt(jnp.mean(xf * xf, axis=-1, keepdims=True) + EPS)
    return (y * g.astype(jnp.float32)).astype(x.dtype)


def layernorm(x, g, b):
    xf = x.astype(jnp.float32)
    mu = jnp.mean(xf, axis=-1, keepdims=True)
    var = jnp.mean(jnp.square(xf - mu), axis=-1, keepdims=True)
    y = (xf - mu) * lax.rsqrt(var + EPS)
    return (y * g.astype(jnp.float32) + b.astype(jnp.float32)).astype(x.dtype)


def modulate(h, shift, scale):
    return h * (1 + scale) + shift


def to_heads(t):
    return t.reshape(t.shape[0], t.shape[1], NA_HEADS, NA_HEAD_DIM)


def split_in_proj(h, w_in):
    proj = h @ w_in
    q, k, v, u, z = jnp.split(proj, [NA_WIDTH, 2 * NA_WIDTH, 3 * NA_WIDTH, 3 * NA_WIDTH + SG_WIDTH], axis=-1)
    return to_heads(q), to_heads(k), to_heads(v), jax.nn.gelu(u), jax.nn.gelu(z)


def context_kv(hc, w_in):
    kc, vc = jnp.split(hc @ w_in[:, NA_WIDTH:3 * NA_WIDTH], 2, axis=-1)
    return to_heads(kc), to_heads(vc)


def neighbourhood_attention(q, k, v, kc, vc, rpb):
    B, S, H, Dh = q.shape
    rows = S // GRID_W
    kh = min(WIN_H, rows)
    q = q * (Dh ** -0.5)
    qg = q.reshape(B, rows, GRID_W, H, Dh).transpose(1, 0, 3, 2, 4)
    kg = k.reshape(B, rows, GRID_W, H, Dh).transpose(0, 3, 1, 2, 4)
    vg = v.reshape(B, rows, GRID_W, H, Dh).transpose(0, 3, 1, 2, 4)
    kct = kc.transpose(0, 2, 1, 3)
    vct = vc.transpose(0, 2, 1, 3)
    cols = jnp.arange(GRID_W)
    col_start = jnp.clip(cols - WIN_W // 2, 0, GRID_W - WIN_W)
    col_idx = col_start[:, None] + jnp.arange(WIN_W)[None, :]
    col_off = col_idx - cols[:, None] + (WIN_W - 1)
    n_win = kh * WIN_W

    def row_block(args):
        r, q_r = args
        r0 = jnp.clip(r - kh // 2, 0, rows - kh)
        k_rows = lax.dynamic_slice_in_dim(kg, r0, kh, axis=2)
        v_rows = lax.dynamic_slice_in_dim(vg, r0, kh, axis=2)
        k_win = k_rows[:, :, :, col_idx, :]
        v_win = v_rows[:, :, :, col_idx, :]
        row_off = r0 + jnp.arange(kh) - r + (WIN_H - 1)
        bias = rpb[:, row_off[None, :, None], col_off[:, None, :]]
        s_win = jnp.einsum('bhqd,bhiqjd->bhqij', q_r, k_win) + bias[None]
        s_ctx = jnp.einsum('bhqd,bhcd->bhqc', q_r, kct)
        s = jnp.concatenate([s_win.reshape(B, H, GRID_W, n_win), s_ctx], axis=-1)
        p = jax.nn.softmax(s.astype(jnp.float32), axis=-1).astype(v.dtype)
        p_win = p[..., :n_win].reshape(B, H, GRID_W, kh, WIN_W)
        p_ctx = p[..., n_win:]
        return (jnp.einsum('bhqij,bhiqjd->bhqd', p_win, v_win)
                + jnp.einsum('bhqc,bhcd->bhqd', p_ctx, vct))

    out = lax.map(row_block, (jnp.arange(rows), qg))
    return out.transpose(1, 0, 3, 2, 4).reshape(B, S, H * Dh)


def context_attention(qc, kc, vc):
    B, C, H, Dh = qc.shape
    s = jnp.einsum('bqhd,bkhd->bhqk', qc * (Dh ** -0.5), kc)
    p = jax.nn.softmax(s.astype(jnp.float32), axis=-1).astype(vc.dtype)
    return jnp.einsum('bhqk,bkhd->bqhd', p, vc).reshape(B, C, H * Dh)


def spatial_gating(u, z, ln_g, ln_b, w_s, b_s):
    B, L, _ = z.shape
    z = layernorm(z, ln_g, ln_b)
    zc = z.reshape(B, L // CHUNK, CHUNK, SG_GROUPS, SG_GROUP_DIM)
    mixed = jnp.einsum('gpq,bnqgd->bnpgd', w_s, zc) + b_s.T[:, :, None]
    return u * mixed.reshape(B, L, SG_WIDTH)


def merge_heads(attn, sgu, g_na, g_sg, w_out):
    return jnp.concatenate([rmsnorm(attn, g_na), rmsnorm(sgu, g_sg)], axis=-1) @ w_out


def swiglu(t, w_gate, w_up, w_down):
    return (jax.nn.silu(t @ w_gate) * (t @ w_up)) @ w_down


def moe_ffn(x, w_router, router_bias, we_gate, we_up, we_down, ws_gate, ws_up, ws_down):
    B, L, D = x.shape
    t = x.reshape(B * L, D)
    n = t.shape[0]
    scores = jax.nn.sigmoid((t @ w_router).astype(jnp.float32))
    choice = scores + router_bias.astype(jnp.float32)
    grouped = choice.reshape(n, N_GROUPS, N_EXPERTS // N_GROUPS)
    group_score = lax.top_k(grouped, 2)[0].sum(axis=-1)
    _, top_groups = lax.top_k(group_score, TOPK_GROUPS)
    group_mask = jax.nn.one_hot(top_groups, N_GROUPS).sum(axis=1) > 0
    expert_mask = jnp.repeat(group_mask, N_EXPERTS // N_GROUPS, axis=1)
    _, top_idx = lax.top_k(jnp.where(expert_mask, choice, -jnp.inf), TOP_K)
    w = jnp.take_along_axis(scores, top_idx, axis=-1)
    w = w / jnp.sum(w, axis=-1, keepdims=True) * ROUTED_SCALE
    gates = jnp.sum(jax.nn.one_hot(top_idx, N_EXPERTS, dtype=jnp.float32) * w[..., None], axis=1).astype(x.dtype)
    out = swiglu(t, ws_gate, ws_up, ws_down)
    for e in range(N_EXPERTS):
        out = out + gates[:, e:e + 1] * swiglu(t, we_gate[e], we_up[e], we_down[e])
    return out.reshape(B, L, D)


def setup_inputs(seed: int = 0) -> dict:
    key = jax.random.key(seed)
    ks = iter(jax.random.split(key, 32))
    D, L = D_MODEL, DEPTH

    def nrm(shape, scale):
        return jax.random.normal(next(ks), shape, jnp.float32) * scale

    return {
        "x": nrm((BATCH, SEQ, D), 1.0),
        "c": nrm((BATCH, D), 1.0),
        "ctx": nrm((BATCH, CTX_LEN, D), 1.0),
        "c_ctx": nrm((D,), 1.0),
        "w_ada": nrm((L, D, 6 * D), 0.5 * D ** -0.5),
        "b_ada": nrm((L, 6 * D), 0.02),
        "norm_mix_g": 1.0 + nrm((L, D), 0.02),
        "norm_ffn_g": 1.0 + nrm((L, D), 0.02),
        "w_in": nrm((L, D, IN_WIDTH), D ** -0.5),
        "na_rpb": nrm((L, NA_HEADS, 2 * WIN_H - 1, 2 * WIN_W - 1), 0.1),
        "sg_ln_g": 1.0 + nrm((L, SG_WIDTH), 0.02),
        "sg_ln_b": nrm((L, SG_WIDTH), 0.02),
        "sg_w_s": nrm((L, SG_GROUPS, CHUNK, CHUNK), CHUNK ** -0.5),
        "sg_b_s": 1.0 + nrm((L, SG_GROUPS, CHUNK), 0.01),
        "out_g_na": 1.0 + nrm((L, NA_WIDTH), 0.02),
        "out_g_sg": 1.0 + nrm((L, SG_WIDTH), 0.02),
        "w_out": nrm((L, MIX_WIDTH, D), MIX_WIDTH ** -0.5),
        "w_router": nrm((L, D, N_EXPERTS), D ** -0.5),
        "router_bias": nrm((L, N_EXPERTS), 0.01),
        "we_gate": nrm((L, N_EXPERTS, D, EXPERT_DIM), D ** -0.5),
        "we_up": nrm((L, N_EXPERTS, D, EXPERT_DIM), D ** -0.5),
        "we_down": nrm((L, N_EXPERTS, EXPERT_DIM, D), EXPERT_DIM ** -0.5),
        "ws_gate": nrm((L, D, SHARED_DIM), D ** -0.5),
        "ws_up": nrm((L, D, SHARED_DIM), D ** -0.5),
        "ws_down": nrm((L, SHARED_DIM, D), SHARED_DIM ** -0.5),
        "final_g": 1.0 + nrm((D,), 0.02),
    }


def reference(x, c, ctx, c_ctx, w_ada, b_ada, norm_mix_g, norm_ffn_g, w_in, na_rpb,
              sg_ln_g, sg_ln_b, sg_w_s, sg_b_s, out_g_na, out_g_sg, w_out,
              w_router, router_bias, we_gate, we_up, we_down, ws_gate, ws_up, ws_down, final_g):
    h_ctx = ctx
    for l in range(DEPTH):
        update_ctx = l + 1 < DEPTH
        mod = (jax.nn.silu(c) @ w_ada[l] + b_ada[l])[:, None, :]
        sh_m, sc_m, g_m, sh_f, sc_f, g_f = jnp.split(mod, 6, axis=-1)
        mod_c = jax.nn.silu(c_ctx) @ w_ada[l] + b_ada[l]
        csh_m, csc_m, cg_m, csh_f, csc_f, cg_f = jnp.split(mod_c, 6, axis=-1)

        hx = modulate(rmsnorm(x, norm_mix_g[l]), sh_m, sc_m)
        hc = modulate(rmsnorm(h_ctx, norm_mix_g[l]), csh_m, csc_m)
        q, k, v, u, z = split_in_proj(hx, w_in[l])
        if update_ctx:
            qc, kc, vc, uc, zc = split_in_proj(hc, w_in[l])
        else:
            kc, vc = context_kv(hc, w_in[l])
        attn = neighbourhood_attention(q, k, v, kc, vc, na_rpb[l])
        sgu = spatial_gating(u, z, sg_ln_g[l], sg_ln_b[l], sg_w_s[l], sg_b_s[l])
        x = x + g_m * merge_heads(attn, sgu, out_g_na[l], out_g_sg[l], w_out[l])

        ffn_w = (w_router[l], router_bias[l], we_gate[l], we_up[l], we_down[l], ws_gate[l], ws_up[l], ws_down[l])
        hf = modulate(rmsnorm(x, norm_ffn_g[l]), sh_f, sc_f)
        x = x + g_f * moe_ffn(hf, *ffn_w)

        if update_ctx:
            attn_c = context_attention(qc, kc, vc)
            sgu_c = spatial_gating(uc, zc, sg_ln_g[l], sg_ln_b[l], sg_w_s[l], sg_b_s[l])
            h_ctx = h_ctx + cg_m * merge_heads(attn_c, sgu_c, out_g_na[l], out_g_sg[l], w_out[l])
            hcf = modulate(rmsnorm(h_ctx, norm_ffn_g[l]), csh_f, csc_f)
            h_ctx = h_ctx + cg_f * moe_ffn(hcf, *ffn_w)
    return rmsnorm(x, final_g)
```

```python
import functools

import numpy as np
import jax
import jax.numpy as jnp
from jax import lax
from jax.experimental import pallas as pl
from jax.experimental.pallas import tpu as pltpu

F32 = jnp.float32
BF16 = jnp.bfloat16
I32 = jnp.int32

GRID_W = 64
WIN_H = 8
WIN_W = 16
CHUNK = 128
HEAD_DIM = 128
N_GROUPS = 8
TOPK_GROUPS = 4
TOP_K = 8
ROUTED_SCALE = 2.5
EPS = 1e-6

ROWS_PER_BLOCK = WIN_H // 2
Q_BLOCK = ROWS_PER_BLOCK * GRID_W
MASK_VALUE = -1e30
V7X_VMEM_BYTES = 64 * 1024 * 1024
VMEM_LIMIT = V7X_VMEM_BYTES - 8 * 1024 * 1024


def _params(*sem):
    return pltpu.CompilerParams(dimension_semantics=sem, vmem_limit_bytes=VMEM_LIMIT)


def _bf16_bits(x):
    return lax.bitcast_convert_type(x.astype(jnp.bfloat16).astype(F32), I32)


def _pack_halves(x):
    w = x.shape[-1] // 2
    lo = lax.shift_right_logical(_bf16_bits(x[:, :w]), 16)
    hi = _bf16_bits(x[:, w:])
    return hi | lo


def _unpack_halves(p):
    lo = lax.bitcast_convert_type(p << 16, F32)
    hi = lax.bitcast_convert_type(p & jnp.int32(-65536), F32)
    return lo, hi


def _ada_kernel(c_ref, w_ref, b_ref, o_ref):
    s = jax.nn.silu(c_ref[...]).astype(BF16)
    o_ref[...] = jnp.dot(s, w_ref[...].astype(BF16), preferred_element_type=F32) + b_ref[...]


def _ada(c_rows, w_ada, b_ada, tn=512):
    r, d = c_rows.shape
    n = w_ada.shape[1]
    return pl.pallas_call(
        _ada_kernel,
        grid=(n // tn,),
        in_specs=[
            pl.BlockSpec((r, d), lambda j: (0, 0)),
            pl.BlockSpec((d, tn), lambda j: (0, j)),
            pl.BlockSpec((1, tn), lambda j: (0, j)),
        ],
        out_specs=pl.BlockSpec((r, tn), lambda j: (0, j)),
        out_shape=jax.ShapeDtypeStruct((r, n), F32),
        compiler_params=_params("arbitrary"),
        name="ada",
    )(c_rows, w_ada, b_ada.reshape(1, n))


def _norm_mod_kernel(x_ref, g_ref, sh_ref, sc_ref, o_ref):
    x = x_ref[0]
    y = x * lax.rsqrt(jnp.mean(x * x, axis=-1, keepdims=True) + EPS) * g_ref[...]
    o_ref[0] = (y * (1.0 + sc_ref[0]) + sh_ref[0]).astype(o_ref.dtype)


def _norm_mod(x, g, shift, scale, tr):
    b, s, d = x.shape
    return pl.pallas_call(
        _norm_mod_kernel,
        grid=(b, s // tr),
        in_specs=[
            pl.BlockSpec((1, tr, d), lambda i, j: (i, j, 0)),
            pl.BlockSpec((1, d), lambda i, j: (0, 0)),
            pl.BlockSpec((1, 1, d), lambda i, j: (i, 0, 0)),
            pl.BlockSpec((1, 1, d), lambda i, j: (i, 0, 0)),
        ],
        out_specs=pl.BlockSpec((1, tr, d), lambda i, j: (i, j, 0)),
        out_shape=jax.ShapeDtypeStruct((b, s, d), BF16),
        compiler_params=_params("arbitrary", "arbitrary"),
        name="norm_mod",
    )(x, g.reshape(1, d), shift.reshape(b, 1, d), scale.reshape(b, 1, d))


def _in_proj_kernel(a_ref, w_ref, o_ref, *, q_tiles, gelu_tile0, q_scale):
    j = pl.program_id(1)
    acc = jnp.dot(a_ref[...], w_ref[...], preferred_element_type=F32)

    @pl.when(j < q_tiles)
    def _():
        o_ref[...] = (acc * q_scale).astype(o_ref.dtype)

    @pl.when((j >= q_tiles) & (j < gelu_tile0))
    def _():
        o_ref[...] = acc.astype(o_ref.dtype)

    @pl.when(j >= gelu_tile0)
    def _():
        o_ref[...] = jax.nn.gelu(acc).astype(o_ref.dtype)


def _in_proj(a, w, n_out, col0, q_cols, gelu_col0, tm, tn):
    m, k = a.shape
    kern = functools.partial(
        _in_proj_kernel, q_tiles=q_cols // tn, gelu_tile0=gelu_col0 // tn, q_scale=HEAD_DIM ** -0.5
    )
    cb0 = col0 // tn
    return pl.pallas_call(
        kern,
        grid=(m // tm, n_out // tn),
        in_specs=[
            pl.BlockSpec((tm, k), lambda i, j: (i, 0)),
            pl.BlockSpec((k, tn), lambda i, j: (0, j + cb0)),
        ],
        out_specs=pl.BlockSpec((tm, tn), lambda i, j: (i, j)),
        out_shape=jax.ShapeDtypeStruct((m, n_out), BF16),
        compiler_params=_params("arbitrary", "arbitrary"),
        name="in_proj",
    )(a, w)


def _attn_bias_tables(rpb, rows):
    r_blk = ROWS_PER_BLOCK
    kh = WIN_H
    lr = np.arange(r_blk)[:, None, None, None]
    c = np.arange(GRID_W)[None, :, None, None]
    kr = np.arange(3 * r_blk)[None, None, :, None]
    j = np.arange(GRID_W)[None, None, None, :]
    cs = np.clip(c - WIN_W // 2, 0, GRID_W - WIN_W)
    vcol = (j >= cs) & (j < cs + WIN_W)
    dc = np.clip(j - c + (WIN_W - 1), 0, 2 * WIN_W - 2)
    shape = (r_blk, GRID_W, 3 * r_blk, GRID_W)
    flat = (r_blk * GRID_W, 3 * r_blk * GRID_W)
    tables = []
    for a in (0, r_blk, rows - r_blk):
        r = a + lr
        r0 = np.clip(r - kh // 2, 0, rows - kh)
        i = a - r_blk + kr
        vrow = (i >= r0) & (i < r0 + kh) & (i >= 0) & (i < rows)
        dr = np.clip(i - r + (WIN_H - 1), 0, 2 * WIN_H - 2)
        valid = np.broadcast_to(vrow & vcol, shape).reshape(flat)
        dr_b = np.broadcast_to(dr, shape).reshape(flat)
        dc_b = np.broadcast_to(dc, shape).reshape(flat)
        t = rpb[:, dr_b, dc_b]
        tables.append(jnp.where(valid[None], t, MASK_VALUE))
    return jnp.stack(tables).astype(F32)


def _attn_kernel(q_ref, kp_ref, kc_ref, kn_ref, vp_ref, vc_ref, vn_ref, kx_ref, vx_ref, bias_ref, o_ref):
    q = q_ref[0]
    nt = (((1,), (1,)), ((), ()))
    qb = Q_BLOCK
    keys = (kp_ref[0], kc_ref[0], kn_ref[0])
    vals = (vp_ref[0], vc_ref[0], vn_ref[0])
    scores = [
        lax.dot_general(q, keys[t], nt, preferred_element_type=F32) + bias_ref[0, 0, :, t * qb:(t + 1) * qb]
        for t in range(3)
    ]
    scores.append(lax.dot_general(q, kx_ref[0], nt, preferred_element_type=F32))
    m = functools.reduce(jnp.maximum, [jnp.max(s, axis=-1, keepdims=True) for s in scores])
    probs = [jnp.exp(s - m) for s in scores]
    denom = functools.reduce(jnp.add, [jnp.sum(p, axis=-1, keepdims=True) for p in probs])
    values = vals + (vx_ref[0],)
    out = functools.reduce(
        jnp.add,
        [jnp.dot(p.astype(BF16), v, preferred_element_type=F32) for p, v in zip(probs, values)],
    )
    o_ref[0] = (out / denom).astype(o_ref.dtype)


def _attention(proj, ctx_kv, bias, n_heads):
    b, s, _ = proj.shape
    c_len = ctx_kv.shape[1]
    nrb = s // Q_BLOCK
    h = n_heads
    hd = HEAD_DIM

    def blk_type(rb):
        return jnp.where(rb == 0, 0, jnp.where(rb == nrb - 1, 2, 1))

    def prev(rb):
        return jnp.maximum(rb - 1, 0)

    def nxt(rb):
        return jnp.minimum(rb + 1, nrb - 1)

    row_blk = (1, Q_BLOCK, hd)
    in_specs = [
        pl.BlockSpec(row_blk, lambda hh, bb, rb: (bb, rb, hh)),
        pl.BlockSpec(row_blk, lambda hh, bb, rb: (bb, prev(rb), h + hh)),
        pl.BlockSpec(row_blk, lambda hh, bb, rb: (bb, rb, h + hh)),
        pl.BlockSpec(row_blk, lambda hh, bb, rb: (bb, nxt(rb), h + hh)),
        pl.BlockSpec(row_blk, lambda hh, bb, rb: (bb, prev(rb), 2 * h + hh)),
        pl.BlockSpec(row_blk, lambda hh, bb, rb: (bb, rb, 2 * h + hh)),
        pl.BlockSpec(row_blk, lambda hh, bb, rb: (bb, nxt(rb), 2 * h + hh)),
        pl.BlockSpec((1, c_len, hd), lambda hh, bb, rb: (bb, 0, hh)),
        pl.BlockSpec((1, c_len, hd), lambda hh, bb, rb: (bb, 0, h + hh)),
        pl.BlockSpec((1, 1, Q_BLOCK, 3 * Q_BLOCK), lambda hh, bb, rb: (blk_type(rb), hh, 0, 0)),
    ]
    return pl.pallas_call(
        _attn_kernel,
        grid=(h, b, nrb),
        in_specs=in_specs,
        out_specs=pl.BlockSpec(row_blk, lambda hh, bb, rb: (bb, rb, hh)),
        out_shape=jax.ShapeDtypeStruct((b, s, h * hd), BF16),
        compiler_params=_params("arbitrary", "arbitrary", "arbitrary"),
        name="attn",
    )(proj, proj, proj, proj, proj, proj, proj, ctx_kv, ctx_kv, bias)


def _sgu_kernel(u_ref, z_ref, g_ref, b_ref, ws_ref, bs_ref, o_ref, *, n_groups):
    z = z_ref[0].astype(F32)
    mu = jnp.mean(z, axis=-1, keepdims=True)
    zc = z - mu
    var = jnp.mean(zc * zc, axis=-1, keepdims=True)
    zn = (zc * lax.rsqrt(var + EPS) * g_ref[...] + b_ref[...]).astype(BF16)
    gd = HEAD_DIM
    for g in range(n_groups):
        sl = slice(g * gd, (g + 1) * gd)
        mixed = jnp.dot(ws_ref[g], zn[:, sl], preferred_element_type=F32) + bs_ref[:, g:g + 1]
        o_ref[0, :, sl] = (u_ref[0, :, sl].astype(F32) * mixed).astype(o_ref.dtype)


def _sgu(proj, u_col0, ln_g, ln_b, w_s, b_s):
    b, s, _ = proj.shape
    n_groups = w_s.shape[0]
    width = n_groups * HEAD_DIM
    ub = u_col0 // width
    kern = functools.partial(_sgu_kernel, n_groups=n_groups)
    return pl.pallas_call(
        kern,
        grid=(b, s // CHUNK),
        in_specs=[
            pl.BlockSpec((1, CHUNK, width), lambda i, j: (i, j, ub)),
            pl.BlockSpec((1, CHUNK, width), lambda i, j: (i, j, ub + 1)),
            pl.BlockSpec((1, width), lambda i, j: (0, 0)),
            pl.BlockSpec((1, width), lambda i, j: (0, 0)),
            pl.BlockSpec((n_groups, CHUNK, CHUNK), lambda i, j: (0, 0, 0)),
            pl.BlockSpec((CHUNK, n_groups), lambda i, j: (0, 0)),
        ],
        out_specs=pl.BlockSpec((1, CHUNK, width), lambda i, j: (i, j, 0)),
        out_shape=jax.ShapeDtypeStruct((b, s, width), BF16),
        compiler_params=_params("arbitrary", "arbitrary"),
        name="sgu",
    )(proj, proj, ln_g.reshape(1, width), ln_b.reshape(1, width), w_s.astype(BF16), b_s.T)


def _out_proj_kernel(at_ref, sg_ref, gna_ref, gsg_ref, w_ref, x_ref, gate_ref, o_ref, a_scr):
    j = pl.program_id(1)

    @pl.when(j == 0)
    def _():
        wa = at_ref.shape[-1]
        for src, gain, col in ((at_ref, gna_ref, 0), (sg_ref, gsg_ref, wa)):
            t = src[...].astype(F32)
            y = t * lax.rsqrt(jnp.mean(t * t, axis=-1, keepdims=True) + EPS) * gain[...]
            a_scr[:, col:col + t.shape[-1]] = y.astype(BF16)

    acc = jnp.dot(a_scr[...], w_ref[...], preferred_element_type=F32)
    o_ref[...] = x_ref[...] + gate_ref[0] * acc


def _out_proj(attn, sgu, g_na, g_sg, w_out, x, gate, seq, tm, tn):
    m, wa = attn.shape
    ws = sgu.shape[1]
    k, n = w_out.shape
    tiles_per_batch = seq // tm
    return pl.pallas_call(
        _out_proj_kernel,
        grid=(m // tm, n // tn),
        in_specs=[
            pl.BlockSpec((tm, wa), lambda i, j: (i, 0)),
            pl.BlockSpec((tm, ws), lambda i, j: (i, 0)),
            pl.BlockSpec((1, wa), lambda i, j: (0, 0)),
            pl.BlockSpec((1, ws), lambda i, j: (0, 0)),
            pl.BlockSpec((k, tn), lambda i, j: (0, j)),
            pl.BlockSpec((tm, tn), lambda i, j: (i, j)),
            pl.BlockSpec((1, 1, tn), lambda i, j: (i // tiles_per_batch, 0, j)),
        ],
        out_specs=pl.BlockSpec((tm, tn), lambda i, j: (i, j)),
        out_shape=jax.ShapeDtypeStruct((m, n), F32),
        scratch_shapes=[pltpu.VMEM((tm, k), BF16)],
        compiler_params=_params("arbitrary", "arbitrary"),
        name="out_proj",
    )(attn, sgu, g_na.reshape(1, wa), g_sg.reshape(1, ws), w_out, x, gate.reshape(-1, 1, n))


def _router_kernel(x_ref, g_ref, sh_ref, sc_ref, wr_ref, rb_ref,
                   hf_ref, ek_ref, wk_ref, pk_ref, cnt_ref, carry, *, n_experts):
    i = pl.program_id(0)

    @pl.when(i == 0)
    def _():
        carry[...] = jnp.zeros_like(carry)

    x = x_ref[...]
    t = x.shape[0]
    y = x * lax.rsqrt(jnp.mean(x * x, axis=-1, keepdims=True) + EPS) * g_ref[...]
    hf = y * (1.0 + sc_ref[0]) + sh_ref[0]
    hf_ref[...] = _pack_halves(hf)

    logits = lax.dot_general(wr_ref[...], hf.astype(BF16), (((1,), (1,)), ((), ())),
                             preferred_element_type=F32)
    scores = jax.nn.sigmoid(logits)
    choice = scores + rb_ref[...]
    per_group = n_experts // N_GROUPS
    shp = (N_GROUPS, per_group, t)
    c3 = choice.reshape(shp)
    s3 = scores.reshape(shp)
    gi = lax.broadcasted_iota(I32, shp, 0)
    ei = lax.broadcasted_iota(I32, shp, 1)
    eid = gi * per_group + ei
    neg_inf = jnp.float32(-jnp.inf)

    m1 = jnp.max(c3, axis=1, keepdims=True)
    first = jnp.min(jnp.where(c3 == m1, ei, per_group), axis=1, keepdims=True)
    m2 = jnp.max(jnp.where(ei == first, neg_inf, c3), axis=1, keepdims=True)
    gs = m1 + m2
    gidx = lax.broadcasted_iota(I32, gs.shape, 0)
    gsel = jnp.zeros(gs.shape, jnp.bool_)
    for _ in range(TOPK_GROUPS):
        gm = jnp.max(gs, axis=0, keepdims=True)
        pick = jnp.min(jnp.where(gs == gm, gidx, N_GROUPS), axis=0, keepdims=True)
        hit = gidx == pick
        gsel = gsel | hit
        gs = jnp.where(hit, neg_inf, gs)

    masked = jnp.where(gsel, c3, neg_inf)
    sel = jnp.zeros(shp, jnp.bool_)
    picks = []
    for _ in range(TOP_K):
        mx = jnp.max(jnp.max(masked, axis=1, keepdims=True), axis=0, keepdims=True)
        cand = jnp.where(masked == mx, eid, n_experts)
        pick = jnp.min(jnp.min(cand, axis=1, keepdims=True), axis=0, keepdims=True)
        hit = eid == pick
        sel = sel | hit
        masked = jnp.where(hit, neg_inf, masked)
        picks.append(pick)

    sel2 = sel.reshape(n_experts, t)
    sel_b = jnp.where(sel2, 1.0, 0.0).astype(BF16)
    tri = (lax.broadcasted_iota(I32, (t, t), 0) < lax.broadcasted_iota(I32, (t, t), 1))
    prefix = jnp.dot(sel_b, jnp.where(tri, 1.0, 0.0).astype(BF16), preferred_element_type=F32)
    pos3 = (carry[...] + prefix).reshape(shp)
    carry[...] = carry[...] + jnp.sum(jnp.where(sel2, 1.0, 0.0), axis=1, keepdims=True)
    cnt_ref[...] = jnp.broadcast_to(carry[...], cnt_ref.shape).astype(I32)

    def pick_sum(v3, hit):
        return jnp.sum(jnp.sum(jnp.where(hit, v3, 0.0), axis=1, keepdims=True), axis=0, keepdims=True)

    ws = []
    for k in range(TOP_K):
        hit = eid == picks[k]
        ws.append(pick_sum(s3, hit).reshape(1, t))
        ek_ref[k:k + 1, :] = picks[k].reshape(1, t)
        pk_ref[k:k + 1, :] = pick_sum(pos3, hit).reshape(1, t).astype(I32)
    wsum = functools.reduce(jnp.add, ws)
    for k in range(TOP_K):
        wk_ref[k:k + 1, :] = ws[k] / wsum * ROUTED_SCALE


def _router(x1, g, shift, scale, w_router, router_bias, seq, tr):
    n, d = x1.shape
    e = w_router.shape[1]
    b = shift.shape[0]
    tiles_per_batch = seq // tr
    kern = functools.partial(_router_kernel, n_experts=e)
    tok = lambda i: (0, i)
    return pl.pallas_call(
        kern,
        grid=(n // tr,),
        in_specs=[
            pl.BlockSpec((tr, d), lambda i: (i, 0)),
            pl.BlockSpec((1, d), lambda i: (0, 0)),
            pl.BlockSpec((1, 1, d), lambda i: (i // tiles_per_batch, 0, 0)),
            pl.BlockSpec((1, 1, d), lambda i: (i // tiles_per_batch, 0, 0)),
            pl.BlockSpec((e, d), lambda i: (0, 0)),
            pl.BlockSpec((e, 1), lambda i: (0, 0)),
        ],
        out_specs=[
            pl.BlockSpec((tr, d // 2), lambda i: (i, 0)),
            pl.BlockSpec((TOP_K, tr), tok),
            pl.BlockSpec((TOP_K, tr), tok),
            pl.BlockSpec((TOP_K, tr), tok),
            pl.BlockSpec((e, 128), lambda i: (0, 0)),
        ],
        out_shape=[
            jax.ShapeDtypeStruct((n, d // 2), I32),
            jax.ShapeDtypeStruct((TOP_K, n), I32),
            jax.ShapeDtypeStruct((TOP_K, n), F32),
            jax.ShapeDtypeStruct((TOP_K, n), I32),
            jax.ShapeDtypeStruct((e, 128), I32),
        ],
        scratch_shapes=[pltpu.VMEM((e, 1), F32)],
        compiler_params=_params("arbitrary"),
        name="router",
    )(x1, g.reshape(1, d), shift.reshape(b, 1, d), scale.reshape(b, 1, d),
      w_router.T.astype(BF16), router_bias.reshape(e, 1))


def _dispatch_kernel(pad_lo_ref, pad_hi_ref, nu_ref, slot_ref, x_ref, xs_ref, ztile, sem, *,
                     n_experts, rows_per_step):
    i = pl.program_id(0)
    tm = ztile.shape[0]
    n_tiles = xs_ref.shape[0] // tm

    def row_copy(t, k):
        dst = slot_ref[t * TOP_K + k]
        return pltpu.make_async_copy(x_ref.at[pl.ds(t, 1)], xs_ref.at[pl.ds(dst, 1)], sem)

    def pad_copy(s):
        return pltpu.make_async_copy(ztile.at[pl.ds(0, 1)], xs_ref.at[pl.ds(s, 1)], sem)

    def tail_copy(tile):
        return pltpu.make_async_copy(ztile, xs_ref.at[pl.ds(pl.multiple_of(tile * tm, tm), tm)], sem)

    @pl.when(i == 0)
    def _():
        ztile[...] = jnp.zeros_like(ztile)
        for phase in ("start", "wait"):
            def per_expert(e, carry):
                def per_slot(s, c):
                    cp = pad_copy(s)
                    cp.start() if phase == "start" else cp.wait()
                    return c
                return lax.fori_loop(pad_lo_ref[e], pad_hi_ref[e], per_slot, carry)
            lax.fori_loop(0, n_experts, per_expert, 0)

            def per_tile(tile, c):
                cp = tail_copy(tile)
                cp.start() if phase == "start" else cp.wait()
                return c
            lax.fori_loop(nu_ref[0], n_tiles, per_tile, 0)

    def start_row(t, c):
        for k in range(TOP_K):
            row_copy(t, k).start()
        return c

    def wait_row(t, c):
        for k in range(TOP_K):
            row_copy(t, k).wait()
        return c

    lax.fori_loop(0, rows_per_step, start_row, 0)
    lax.fori_loop(0, rows_per_step, wait_row, 0)


def _dispatch(hf_packed, slot_flat, pad_lo, pad_hi, n_used, n_tiles, tm, td):
    n, w = hf_packed.shape
    e = pad_lo.shape[0]
    kern = functools.partial(_dispatch_kernel, n_experts=e, rows_per_step=td)
    grid_spec = pltpu.PrefetchScalarGridSpec(
        num_scalar_prefetch=3,
        grid=(n // td,),
        in_specs=[
            pl.BlockSpec((td * TOP_K,), lambda i, lo, hi, nu: (i,), memory_space=pltpu.SMEM),
            pl.BlockSpec((td, w), lambda i, lo, hi, nu: (i, 0)),
        ],
        out_specs=pl.BlockSpec(memory_space=pl.ANY),
        scratch_shapes=[pltpu.VMEM((tm, w), I32), pltpu.SemaphoreType.DMA(())],
    )
    return pl.pallas_call(
        kern,
        grid_spec=grid_spec,
        out_shape=jax.ShapeDtypeStruct((n_tiles * tm, w), I32),
        compiler_params=_params("arbitrary"),
        name="dispatch",
    )(pad_lo, pad_hi, n_used, slot_flat, hf_packed)


def _expert_kernel(te_ref, nu_ref, x_ref, wg_ref, wu_ref, wd_ref, o_ref):
    i = pl.program_id(0)

    @pl.when(i < nu_ref[0])
    def _():
        lo, hi = _unpack_halves(x_ref[...])
        lo = lo.astype(BF16)
        hi = hi.astype(BF16)
        half = lo.shape[-1]

        def proj(w_ref):
            return (jnp.dot(lo, w_ref[0, :half, :], preferred_element_type=F32)
                    + jnp.dot(hi, w_ref[0, half:, :], preferred_element_type=F32))

        hmid = (jax.nn.silu(proj(wg_ref)) * proj(wu_ref)).astype(BF16)
        o_ref[...] = _pack_halves(jnp.dot(hmid, wd_ref[0], preferred_element_type=F32))

    @pl.when(i >= nu_ref[0])
    def _():
        o_ref[...] = jnp.zeros_like(o_ref)


def _experts(xs, tile_expert, n_used, w_gate, w_up, w_down, tm):
    p, w = xs.shape
    _, d, f = w_gate.shape

    def row_map(i, te, nu):
        return (jnp.minimum(i, nu[0] - 1), 0)

    grid_spec = pltpu.PrefetchScalarGridSpec(
        num_scalar_prefetch=2,
        grid=(p // tm,),
        in_specs=[
            pl.BlockSpec((tm, w), row_map),
            pl.BlockSpec((1, d, f), lambda i, te, nu: (te[i], 0, 0)),
            pl.BlockSpec((1, d, f), lambda i, te, nu: (te[i], 0, 0)),
            pl.BlockSpec((1, f, d), lambda i, te, nu: (te[i], 0, 0)),
        ],
        out_specs=pl.BlockSpec((tm, w), lambda i, te, nu: (i, 0)),
    )
    return pl.pallas_call(
        _expert_kernel,
        grid_spec=grid_spec,
        out_shape=jax.ShapeDtypeStruct((p, w), I32),
        compiler_params=_params("arbitrary"),
        name="experts",
    )(tile_expert, n_used, xs, w_gate, w_up, w_down)


def _combine_kernel(slot_ref, x_ref, ysh_ref, w_ref, gate_ref, fg_ref, y_ref, o_ref, ybuf, sem, *, rows_per_step):
    def row_copy(t, k):
        src = slot_ref[t * TOP_K + k]
        return pltpu.make_async_copy(y_ref.at[pl.ds(src, 1)], ybuf.at[k, pl.ds(t, 1)], sem)

    def start_row(t, c):
        for k in range(TOP_K):
            row_copy(t, k).start()
        return c

    def wait_row(t, c):
        for k in range(TOP_K):
            row_copy(t, k).wait()
        return c

    lax.fori_loop(0, rows_per_step, start_row, 0)
    lax.fori_loop(0, rows_per_step, wait_row, 0)

    lo, hi = _unpack_halves(ysh_ref[...])
    for k in range(TOP_K):
        wk = w_ref[:, k:k + 1]
        l, h = _unpack_halves(ybuf[k])
        lo = lo + wk * l
        hi = hi + wk * h
    half = lo.shape[-1]
    gate = gate_ref[0]
    out_lo = x_ref[:, :half] + gate[:, :half] * lo
    out_hi = x_ref[:, half:] + gate[:, half:] * hi
    ms = (jnp.sum(out_lo * out_lo, axis=-1, keepdims=True)
          + jnp.sum(out_hi * out_hi, axis=-1, keepdims=True)) / (2 * half)
    inv = lax.rsqrt(ms + EPS)
    o_ref[:, :half] = out_lo * inv * fg_ref[:, :half]
    o_ref[:, half:] = out_hi * inv * fg_ref[:, half:]


def _combine(x1, y_shared, y_sorted, slot_flat, w_tok, gate, final_g, seq, tc):
    n, d = x1.shape
    w = d // 2
    tiles_per_batch = seq // tc
    kern = functools.partial(_combine_kernel, rows_per_step=tc)
    return pl.pallas_call(
        kern,
        grid=(n // tc,),
        in_specs=[
            pl.BlockSpec((tc * TOP_K,), lambda i: (i,), memory_space=pltpu.SMEM),
            pl.BlockSpec((tc, d), lambda i: (i, 0)),
            pl.BlockSpec((tc, w), lambda i: (i, 0)),
            pl.BlockSpec((tc, TOP_K), lambda i: (i, 0)),
            pl.BlockSpec((1, 1, d), lambda i: (i // tiles_per_batch, 0, 0)),
            pl.BlockSpec((1, d), lambda i: (0, 0)),
            pl.BlockSpec(memory_space=pl.ANY),
        ],
        out_specs=pl.BlockSpec((tc, d), lambda i: (i, 0)),
        out_shape=jax.ShapeDtypeStruct((n, d), F32),
        scratch_shapes=[pltpu.VMEM((TOP_K, tc, w), I32), pltpu.SemaphoreType.DMA(())],
        compiler_params=_params("arbitrary"),
        name="combine",
    )(slot_flat, x1, y_shared, w_tok, gate.reshape(-1, 1, d), final_g.reshape(1, d), y_sorted)


def _tile(n, pref):
    t = min(n, pref)
    assert n % t == 0, (n, pref)
    return t


def kernel(x, c, ctx, c_ctx, w_ada, b_ada, norm_mix_g, norm_ffn_g, w_in, na_rpb, sg_ln_g, sg_ln_b, sg_w_s, sg_b_s, out_g_na, out_g_sg, w_out, w_router, router_bias, we_gate, we_up, we_down, ws_gate, ws_up, ws_down, final_g):
    b, s, d = x.shape
    c_len = ctx.shape[1]
    depth = w_ada.shape[0]
    assert depth == 1, "context stream update for deeper stacks is not implemented"
    n_heads = na_rpb.shape[1]
    na_w = n_heads * HEAD_DIM
    sg_w = sg_w_s.shape[1] * HEAD_DIM
    n_exp = w_router.shape[-1]
    rows = s // GRID_W
    assert s % Q_BLOCK == 0 and rows >= 3 * ROWS_PER_BLOCK and c_len % 8 == 0
    n = b * s

    c_rows = jnp.concatenate([c, c_ctx[None], jnp.zeros((-(b + 1) % 8, d), F32)], axis=0)
    mod = _ada(c_rows, w_ada[0], b_ada[0], tn=_tile(6 * d, 512))
    sh_m, sc_m, g_m, sh_f, sc_f, g_f = [mod[:b, t * d:(t + 1) * d] for t in range(6)]
    csh_m, csc_m = [jnp.broadcast_to(mod[b:b + 1, t * d:(t + 1) * d], (b, d)) for t in range(2)]

    w_in_b = w_in[0].astype(BF16)
    hx = _norm_mod(x, norm_mix_g[0], sh_m, sc_m, tr=_tile(s, 512))
    hc = _norm_mod(ctx, norm_mix_g[0], csh_m, csc_m, tr=_tile(c_len, 256))
    in_w = w_in_b.shape[1]
    proj = _in_proj(hx.reshape(n, d), w_in_b, in_w, 0, na_w, 3 * na_w, tm=_tile(n, 1024), tn=_tile(na_w, 512))
    ctx_kv = _in_proj(hc.reshape(b * c_len, d), w_in_b, 2 * na_w, na_w, 0, 2 * na_w,
                      tm=_tile(b * c_len, 1024), tn=_tile(na_w, 512))
    proj = proj.reshape(b, s, in_w)
    bias = _attn_bias_tables(na_rpb[0], rows)
    attn = _attention(proj, ctx_kv.reshape(b, c_len, 2 * na_w), bias, n_heads)
    sgu = _sgu(proj, 3 * na_w, sg_ln_g[0], sg_ln_b[0], sg_w_s[0], sg_b_s[0])
    x1 = _out_proj(attn.reshape(n, na_w), sgu.reshape(n, sg_w), out_g_na[0], out_g_sg[0],
                   w_out[0].astype(BF16), x.reshape(n, d), g_m, s, tm=_tile(s, 512), tn=_tile(d, 512))

    hf, ek, wk, pk, cnt = _router(x1, norm_ffn_g[0], sh_f, sc_f, w_router[0], router_bias[0], s, tr=_tile(s, 256))
    tm = 256
    counts = cnt[:, 0]
    padded = (counts + tm - 1) // tm * tm
    ends = jnp.cumsum(padded)
    off = ends - padded
    n_tiles = (n * TOP_K) // tm + n_exp
    tile_expert = jnp.minimum(
        jnp.sum(jnp.arange(n_tiles, dtype=I32)[:, None] >= (ends // tm)[None, :], axis=1), n_exp - 1
    ).astype(I32)
    n_used = (ends[-1:] // tm).astype(I32)
    slot_flat = (off[ek] + pk).T.reshape(n * TOP_K).astype(I32)
    td = _tile(s, 128)
    xs = _dispatch(hf, slot_flat, (off + counts).astype(I32), ends.astype(I32), n_used, n_tiles, tm, td)

    y_sorted = _experts(xs, tile_expert, n_used, we_gate[0].astype(BF16), we_up[0].astype(BF16),
                        we_down[0].astype(BF16), tm)
    ts = _tile(n, 512)
    y_shared = _experts(hf, jnp.zeros((n // ts,), I32), jnp.full((1,), n // ts, I32),
                        ws_gate.astype(BF16), ws_up.astype(BF16), ws_down.astype(BF16), ts)
    out = _combine(x1, y_shared, y_sorted, slot_flat, wk.T, g_f, final_g, s, td)
    return out.reshape(b, s, d)
```
